```python
import jax, jax.numpy as jnp
from jax import lax
import numpy as np

D_MODEL = 1024
BATCH = 8
SEQ = 4096
DEPTH = 1

D_FF = 2816
D_A = D_MODEL
D_B = D_MODEL
GROUP = 128
CONV_A = 31
CONV_B = 3
EPS = 1e-6
SPLITS = (D_A, 2 * D_A, 2 * D_A + D_B, 2 * D_A + 2 * D_B, 2 * D_A + 3 * D_B, 2 * D_A + 3 * D_B + D_MODEL)
D_IN = 2 * D_A + 3 * D_B + 2 * D_MODEL

kernel_name = "macaron_gated_conformer_shortconv_hybrid"


def rmsnorm(x, g):
    xf = x.astype(jnp.float32)
    y = xf * lax.rsqrt(jnp.mean(xf * xf, axis=-1, keepdims=True) + EPS)
    return (y * g.astype(jnp.float32)).astype(x.dtype)


def layernorm(x, g, b):
    xf = x.astype(jnp.float32)
    mu = jnp.mean(xf, axis=-1, keepdims=True)
    var = jnp.mean(jnp.square(xf - mu), axis=-1, keepdims=True)
    y = (xf - mu) * lax.rsqrt(var + EPS)
    return (y * g.astype(jnp.float32) + b.astype(jnp.float32)).astype(x.dtype)


def swiglu(x, w_gate, w_up, w_down):
    return (jax.nn.silu(x @ w_gate) * (x @ w_up)) @ w_down


def causal_depthwise_conv(x, w):
    k, c = w.shape
    return lax.conv_general_dilated(
        x, w[:, None, :].astype(x.dtype), window_strides=(1,), padding=((k - 1, 0),),
        dimension_numbers=("NWC", "WIO", "NWC"), feature_group_count=c)


def setup_inputs(seed: int = 0) -> dict:
    key = jax.random.key(seed)
    ks = jax.random.split(key, 24)
    f32 = jnp.float32

    def nrm(k, shape, fan_in):
        return jax.random.normal(k, shape, f32) * (fan_in ** -0.5)

    def gain(k, shape):
        return 1.0 + 0.01 * jax.random.normal(k, shape, f32)

    L = DEPTH
    return {
        "x": jax.random.normal(ks[0], (BATCH, SEQ, D_MODEL), f32),
        "ffn1_norm": gain(ks[1], (L, D_MODEL)),
        "ffn1_w_gate": nrm(ks[2], (L, D_MODEL, D_FF), D_MODEL),
        "ffn1_w_up": nrm(ks[3], (L, D_MODEL, D_FF), D_MODEL),
        "ffn1_w_down": nrm(ks[4], (L, D_FF, D_MODEL), D_FF),
        "mix_norm": gain(ks[5], (L, D_MODEL)),
        "w_in": nrm(ks[6], (L, D_MODEL, D_IN), D_MODEL),
        "a_dw_w": nrm(ks[7], (L, CONV_A, D_A), CONV_A),
        "a_dw_b": 0.01 * jax.random.normal(ks[8], (L, D_A), f32),
        "a_ln_g": gain(ks[9], (L, D_A)),
        "a_ln_b": 0.01 * jax.random.normal(ks[10], (L, D_A), f32),
        "a_w_out": nrm(ks[11], (L, D_A, D_MODEL), D_A),
        "b_conv_w": nrm(ks[12], (L, CONV_B, D_B), CONV_B),
        "b_w_out": nrm(ks[13], (L, D_B, D_MODEL), D_B),
        "w_o": nrm(ks[14], (L, D_MODEL, D_MODEL), D_MODEL),
        "ffn2_norm": gain(ks[15], (L, D_MODEL)),
        "ffn2_w_gate": nrm(ks[16], (L, D_MODEL, D_FF), D_MODEL),
        "ffn2_w_up": nrm(ks[17], (L, D_MODEL, D_FF), D_MODEL),
        "ffn2_w_down": nrm(ks[18], (L, D_FF, D_MODEL), D_FF),
        "final_norm": gain(ks[19], (D_MODEL,)),
    }


def reference(x, ffn1_norm, ffn1_w_gate, ffn1_w_up, ffn1_w_down, mix_norm, w_in,
              a_dw_w, a_dw_b, a_ln_g, a_ln_b, a_w_out, b_conv_w, b_w_out, w_o,
              ffn2_norm, ffn2_w_gate, ffn2_w_up, ffn2_w_down, final_norm):
    h = x
    for l in range(DEPTH):
        h = h + 0.5 * swiglu(rmsnorm(h, ffn1_norm[l]), ffn1_w_gate[l], ffn1_w_up[l], ffn1_w_down[l])

        u = rmsnorm(h, mix_norm[l])
        z = u @ w_in[l]
        a_val, a_gate, b_B, b_C, b_x, g_a, g_b = jnp.split(z, SPLITS, axis=-1)

        a = a_val * jax.nn.sigmoid(a_gate)
        a = causal_depthwise_conv(a, a_dw_w[l]) + a_dw_b[l]
        a = jax.nn.silu(layernorm(a, a_ln_g[l], a_ln_b[l]))
        y_a = a @ a_w_out[l]

        v = causal_depthwise_conv(b_C * b_x, b_conv_w[l])
        y_b = (b_B * v) @ b_w_out[l]

        m = jax.nn.sigmoid(g_a) * y_a + jax.nn.sigmoid(g_b) * y_b
        h = h + m @ w_o[l]

        h = h + 0.5 * swiglu(rmsnorm(h, ffn2_norm[l]), ffn2_w_gate[l], ffn2_w_up[l], ffn2_w_down[l])
    return rmsnorm(h, final_norm)
```

```python
import functools

import jax
import jax.numpy as jnp
from jax import lax
from jax.experimental import pallas as pl
from jax.experimental.pallas import tpu as pltpu

EPS = 1e-6
LANES = 128
HALO = 32
FF_CHUNK = 256
PROJ_CHUNK = 256
CONV_ROWS = 16
VMEM_LIMIT = 60 * 1024 * 1024

F32 = jnp.float32
BF16 = jnp.bfloat16


def _dot(a, b):
    return jnp.dot(a, b, preferred_element_type=F32)


def _rms(x, g):
    return x * lax.rsqrt(jnp.mean(x * x, axis=-1, keepdims=True) + EPS) * g


def _resident(shape):
    return pl.BlockSpec(shape, lambda i: (0,) * len(shape), pipeline_mode=pl.Buffered(1))


def _swiglu_half(xn, wg_ref, wu_ref, wd_ref, act_ref):
    d_ff = wg_ref.shape[1]
    for j in range(d_ff // FF_CHUNK):
        cols = slice(j * FF_CHUNK, (j + 1) * FF_CHUNK)
        g = _dot(xn, wg_ref[:, cols])
        u = _dot(xn, wu_ref[:, cols])
        act_ref[:, cols] = (g * jax.nn.sigmoid(g) * u).astype(BF16)
    return _dot(act_ref[...], wd_ref[...])


def _ffn1_proj_kernel(x_ref, n1_ref, wg_ref, wu_ref, wd_ref, nm_ref, win_ref,
                      h_ref, a_ref, cx_ref, bb_ref, sga_ref, sgb_ref, act_ref):
    d = x_ref.shape[1]
    x = x_ref[...]
    xn = _rms(x, n1_ref[...]).astype(BF16)
    h = x + 0.5 * _swiglu_half(xn, wg_ref, wu_ref, wd_ref, act_ref)
    h_ref[...] = h
    un = _rms(h, nm_ref[...]).astype(BF16)

    def seg(k, c):
        lo = k * d + c * PROJ_CHUNK
        return _dot(un, win_ref[:, lo:lo + PROJ_CHUNK])

    for c in range(d // PROJ_CHUNK):
        cols = slice(c * PROJ_CHUNK, (c + 1) * PROJ_CHUNK)
        a_ref[:, cols] = (seg(0, c) * jax.nn.sigmoid(seg(1, c))).astype(BF16)
        bb_ref[:, cols] = seg(2, c).astype(BF16)
        cx_ref[:, cols] = (seg(3, c) * seg(4, c)).astype(BF16)
        sga_ref[:, cols] = jax.nn.sigmoid(seg(5, c)).astype(BF16)
        sgb_ref[:, cols] = jax.nn.sigmoid(seg(6, c)).astype(BF16)


def _mixer_kernel(tiles_per_seq, h_ref, a_ref, ah_ref, cx_ref, cxh_ref, bb_ref, sga_ref, sgb_ref,
                  wa_ref, ba_ref, lng_ref, lnb_ref, wao_ref, wb_ref, wbo_ref, wo_ref,
                  o_ref, abuf, cxbuf, acta, actb):
    ts, d = h_ref.shape
    nblk = d // LANES
    ka = wa_ref.shape[0]
    kb = wb_ref.shape[0]
    first = (pl.program_id(0) % tiles_per_seq) == 0

    for c in range(nblk):
        cs = slice(c * LANES, (c + 1) * LANES)
        abuf[c, 0:HALO, :] = jnp.where(first, 0.0, ah_ref[:, cs].astype(F32))
        abuf[c, HALO:, :] = a_ref[:, cs].astype(F32)
        cxbuf[c, 0:HALO, :] = jnp.where(first, 0.0, cxh_ref[:, cs].astype(F32))
        cxbuf[c, HALO:, :] = cx_ref[:, cs].astype(F32)

    def rows_step(r, carry):
        r0 = pl.multiple_of(r * CONV_ROWS, CONV_ROWS)
        conv = []
        for c in range(nblk):
            cs = slice(c * LANES, (c + 1) * LANES)
            acc = jnp.broadcast_to(ba_ref[:, cs], (CONV_ROWS, LANES))
            for k in range(ka):
                acc = acc + abuf[c, pl.ds(r0 + (HALO - (ka - 1) + k), CONV_ROWS), :] * wa_ref[k:k + 1, cs]
            conv.append(acc)
            v = cxbuf[c, pl.ds(r0 + (HALO - (kb - 1)), CONV_ROWS), :] * wb_ref[0:1, cs]
            for k in range(1, kb):
                v = v + cxbuf[c, pl.ds(r0 + (HALO - (kb - 1) + k), CONV_ROWS), :] * wb_ref[k:k + 1, cs]
            actb[pl.ds(r0, CONV_ROWS), cs] = (bb_ref[pl.ds(r0, CONV_ROWS), cs].astype(F32) * v).astype(BF16)
        tot = conv[0]
        for c in range(1, nblk):
            tot = tot + conv[c]
        mu = jnp.sum(tot, axis=-1, keepdims=True) * (1.0 / d)
        cen = [x - mu for x in conv]
        sq = cen[0] * cen[0]
        for c in range(1, nblk):
            sq = sq + cen[c] * cen[c]
        rstd = lax.rsqrt(jnp.sum(sq, axis=-1, keepdims=True) * (1.0 / d) + EPS)
        for c in range(nblk):
            cs = slice(c * LANES, (c + 1) * LANES)
            y = cen[c] * rstd * lng_ref[:, cs] + lnb_ref[:, cs]
            acta[pl.ds(r0, CONV_ROWS), cs] = (y * jax.nn.sigmoid(y)).astype(BF16)
        return carry

    lax.fori_loop(0, ts // CONV_ROWS, rows_step, 0)

    ya = _dot(acta[...], wao_ref[...])
    yb = _dot(actb[...], wbo_ref[...])
    m = (sga_ref[...].astype(F32) * ya + sgb_ref[...].astype(F32) * yb).astype(BF16)
    o_ref[...] = h_ref[...] + _dot(m, wo_ref[...])


def _ffn2_norm_kernel(h_ref, n2_ref, wg_ref, wu_ref, wd_ref, nf_ref, o_ref, act_ref):
    h = h_ref[...]
    hn = _rms(h, n2_ref[...]).astype(BF16)
    h = h + 0.5 * _swiglu_half(hn, wg_ref, wu_ref, wd_ref, act_ref)
    o_ref[...] = _rms(h, nf_ref[...])


def _params():
    return pltpu.CompilerParams(dimension_semantics=("arbitrary",), vmem_limit_bytes=VMEM_LIMIT)


def _ffn1_proj(x, n1, wg, wu, wd, nm, win, tm):
    t, d = x.shape
    d_ff = wg.shape[1]
    row = pl.BlockSpec((tm, d), lambda i: (i, 0))
    vec = _resident((1, d))
    return pl.pallas_call(
        _ffn1_proj_kernel,
        grid=(t // tm,),
        in_specs=[row, vec, _resident(wg.shape), _resident(wu.shape), _resident(wd.shape), vec,
                  _resident(win.shape)],
        out_specs=[row] * 6,
        out_shape=[jax.ShapeDtypeStruct((t, d), F32)] + [jax.ShapeDtypeStruct((t, d), BF16)] * 5,
        scratch_shapes=[pltpu.VMEM((tm, d_ff), BF16)],
        compiler_params=_params(),
        name="ffn1_proj",
    )(x, n1, wg, wu, wd, nm, win)


def _mixer(h, a, cx, bb, sga, sgb, wa, ba, lng, lnb, wao, wb, wbo, wo, seq, ts):
    t, d = h.shape
    row = pl.BlockSpec((ts, d), lambda i: (i, 0))
    halo = pl.BlockSpec((HALO, d), lambda i: (jnp.maximum(i * (ts // HALO) - 1, 0), 0))
    vec = _resident((1, d))
    return pl.pallas_call(
        functools.partial(_mixer_kernel, seq // ts),
        grid=(t // ts,),
        in_specs=[row, row, halo, row, halo, row, row, row,
                  _resident(wa.shape), vec, vec, vec, _resident(wao.shape),
                  _resident(wb.shape), _resident(wbo.shape), _resident(wo.shape)],
        out_specs=row,
        out_shape=jax.ShapeDtypeStruct((t, d), F32),
        scratch_shapes=[pltpu.VMEM((d // LANES, HALO + ts, LANES), F32),
                        pltpu.VMEM((d // LANES, HALO + ts, LANES), F32),
                        pltpu.VMEM((ts, d), BF16),
                        pltpu.VMEM((ts, d), BF16)],
        compiler_params=_params(),
        name="mixer",
    )(h, a, a, cx, cx, bb, sga, sgb, wa, ba, lng, lnb, wao, wb, wbo, wo)


def _ffn2_norm(h, n2, wg, wu, wd, nf, tm):
    t, d = h.shape
    d_ff = wg.shape[1]
    row = pl.BlockSpec((tm, d), lambda i: (i, 0))
    vec = _resident((1, d))
    return pl.pallas_call(
        _ffn2_norm_kernel,
        grid=(t // tm,),
        in_specs=[row, vec, _resident(wg.shape), _resident(wu.shape), _resident(wd.shape), vec],
        out_specs=row,
        out_shape=jax.ShapeDtypeStruct((t, d), F32),
        scratch_shapes=[pltpu.VMEM((tm, d_ff), BF16)],
        compiler_params=_params(),
        name="ffn2_norm",
    )(h, n2, wg, wu, wd, nf)


def kernel(x, ffn1_norm, ffn1_w_gate, ffn1_w_up, ffn1_w_down, mix_norm, w_in, a_dw_w, a_dw_b, a_ln_g,
           a_ln_b, a_w_out, b_conv_w, b_w_out, w_o, ffn2_norm, ffn2_w_gate, ffn2_w_up, ffn2_w_down,
           final_norm):
    batch, seq, d = x.shape
    depth = w_in.shape[0]
    assert a_dw_w.shape[1] - 1 <= HALO and b_conv_w.shape[1] - 1 <= HALO
    tm = 512
    assert seq % tm == 0 and tm % HALO == 0 and d % LANES == 0

    def vec(v):
        return v.reshape(1, d).astype(F32)

    h = x.reshape(batch * seq, d)
    for l in range(depth):
        h, a, cx, bb, sga, sgb = _ffn1_proj(
            h, vec(ffn1_norm[l]), ffn1_w_gate[l].astype(BF16), ffn1_w_up[l].astype(BF16),
            ffn1_w_down[l].astype(BF16), vec(mix_norm[l]), w_in[l].astype(BF16), tm)
        h = _mixer(h, a, cx, bb, sga, sgb, a_dw_w[l], vec(a_dw_b[l]), vec(a_ln_g[l]), vec(a_ln_b[l]),
                   a_w_out[l].astype(BF16), b_conv_w[l], b_w_out[l].astype(BF16), w_o[l].astype(BF16),
                   seq, tm)
        last = l == depth - 1
        assert last, "depth > 1 needs an un-normalised ffn2 variant"
        h = _ffn2_norm(h, vec(ffn2_norm[l]), ffn2_w_gate[l].astype(BF16), ffn2_w_up[l].astype(BF16),
                       ffn2_w_down[l].astype(BF16), vec(final_norm), tm)
    return h.reshape(batch, seq, d)
```

```python
import functools

import jax
import jax.numpy as jnp
from jax import lax
from jax.experimental import pallas as pl
from jax.experimental.pallas import tpu as pltpu

EPS = 1e-6
LANES = 128
HALO = 32
FF_CHUNK = 256
PROJ_CHUNK = 256
CONV_ROWS = 16
CONV_GROUP = 48
VMEM_LIMIT = 62 * 1024 * 1024

F32 = jnp.float32
BF16 = jnp.bfloat16


def _dot(a, b):
    return jnp.dot(a, b, preferred_element_type=F32)


def _rms(x, g):
    return x * lax.rsqrt(jnp.mean(x * x, axis=-1, keepdims=True) + EPS) * g


def _resident(shape):
    return pl.BlockSpec(shape, lambda i: (0,) * len(shape), pipeline_mode=pl.Buffered(1))


def _swiglu_chunk(xn, wg_ref, wu_ref, act_ref, j):
    cols = slice(j * FF_CHUNK, (j + 1) * FF_CHUNK)
    g = _dot(xn, wg_ref[:, cols])
    u = _dot(xn, wu_ref[:, cols])
    act_ref[:, cols] = (g * jax.nn.sigmoid(g) * u).astype(BF16)


def _ffn1_proj_kernel(x_ref, n1_ref, wg_ref, wu_ref, wd_ref, nm_ref, win_ref,
                      h_ref, a_ref, cx_ref, bb_ref, sga_ref, sgb_ref, act_ref):
    d = x_ref.shape[1]
    x = x_ref[...]
    xn = _rms(x, n1_ref[...]).astype(BF16)
    for j in range(wg_ref.shape[1] // FF_CHUNK):
        _swiglu_chunk(xn, wg_ref, wu_ref, act_ref, j)
    h = x + 0.5 * _dot(act_ref[...], wd_ref[...])
    h_ref[...] = h
    un = _rms(h, nm_ref[...]).astype(BF16)

    def seg(k, c):
        lo = k * d + c * PROJ_CHUNK
        return _dot(un, win_ref[:, lo:lo + PROJ_CHUNK])

    for c in range(d // PROJ_CHUNK):
        cols = slice(c * PROJ_CHUNK, (c + 1) * PROJ_CHUNK)
        a_ref[:, cols] = (seg(0, c) * jax.nn.sigmoid(seg(1, c))).astype(BF16)
        bb_ref[:, cols] = seg(2, c).astype(BF16)
        cx_ref[:, cols] = (seg(3, c) * seg(4, c)).astype(BF16)
        sga_ref[:, cols] = jax.nn.sigmoid(seg(5, c)).astype(BF16)
        sgb_ref[:, cols] = jax.nn.sigmoid(seg(6, c)).astype(BF16)


def _mix_ffn2_kernel(n_tiles, tiles_per_seq, apply_final,
                     h_ref, sga_ref, sgb_ref, a_ref, ah_ref, cx_ref, cxh_ref, bb_ref,
                     wa_ref, ba_ref, lng_ref, lnb_ref, wao_ref, wb_ref, wbo_ref, wo_ref,
                     n2_ref, wg_ref, wu_ref, wd_ref, nf_ref,
                     o_ref, abuf, cxbuf, cbuf, acta, actb, m_ref, hres, hn, act_ref):
    tm, d = h_ref.shape
    nblk = d // LANES
    ka = wa_ref.shape[0]
    kb = wb_ref.shape[0]
    s = pl.program_id(0)
    first = (jnp.minimum(s, n_tiles - 1) % tiles_per_seq) == 0

    def stage_conv_inputs():
        for c in range(nblk):
            cs = slice(c * LANES, (c + 1) * LANES)
            abuf[c, 0:HALO, :] = jnp.where(first, 0.0, ah_ref[:, cs].astype(F32))
            abuf[c, HALO:, :] = a_ref[:, cs].astype(F32)
            cxbuf[c, 0:HALO, :] = jnp.where(first, 0.0, cxh_ref[:, cs].astype(F32))
            cxbuf[c, HALO:, :] = cx_ref[:, cs].astype(F32)

    def conv_group(r0):
        for c in range(nblk):
            cs = slice(c * LANES, (c + 1) * LANES)
            acc = jnp.broadcast_to(ba_ref[:, cs], (CONV_GROUP, LANES))
            for k in range(ka):
                acc = acc + abuf[c, pl.ds(r0 + (HALO - (ka - 1) + k), CONV_GROUP), :] * wa_ref[k:k + 1, cs]
            cbuf[:, cs] = acc
            v = cxbuf[c, pl.ds(r0 + (HALO - (kb - 1)), CONV_GROUP), :] * wb_ref[0:1, cs]
            for k in range(1, kb):
                v = v + cxbuf[c, pl.ds(r0 + (HALO - (kb - 1) + k), CONV_GROUP), :] * wb_ref[k:k + 1, cs]
            actb[pl.ds(r0, CONV_GROUP), cs] = (bb_ref[pl.ds(r0, CONV_GROUP), cs].astype(F32) * v).astype(BF16)
        for q in range(CONV_GROUP // CONV_ROWS):
            rows = slice(q * CONV_ROWS, (q + 1) * CONV_ROWS)
            conv = [cbuf[rows, c * LANES:(c + 1) * LANES] for c in range(nblk)]
            tot = conv[0]
            for c in range(1, nblk):
                tot = tot + conv[c]
            mu = jnp.sum(tot, axis=-1, keepdims=True) * (1.0 / d)
            cen = [x - mu for x in conv]
            sq = cen[0] * cen[0]
            for c in range(1, nblk):
                sq = sq + cen[c] * cen[c]
            rstd = lax.rsqrt(jnp.sum(sq, axis=-1, keepdims=True) * (1.0 / d) + EPS)
            for c in range(nblk):
                cs = slice(c * LANES, (c + 1) * LANES)
                y = cen[c] * rstd * lng_ref[:, cs] + lnb_ref[:, cs]
                acta[pl.ds(r0 + q * CONV_ROWS, CONV_ROWS), cs] = (y * jax.nn.sigmoid(y)).astype(BF16)

    n_groups = pl.cdiv(tm, CONV_GROUP)

    @pl.when(s == 0)
    def _first_tile_convs():
        stage_conv_inputs()

        def body(j, carry):
            conv_group(pl.multiple_of(jnp.minimum(j * CONV_GROUP, tm - CONV_GROUP), CONV_ROWS))
            return carry

        lax.fori_loop(0, n_groups, body, 0)

    @pl.when(s > 0)
    def _steady_state():
        for c in range(d // PROJ_CHUNK):
            cols = slice(c * PROJ_CHUNK, (c + 1) * PROJ_CHUNK)
            ya = _dot(acta[...], wao_ref[:, cols])
            yb = _dot(actb[...], wbo_ref[:, cols])
            m_ref[:, cols] = (sga_ref[:, cols].astype(F32) * ya + sgb_ref[:, cols].astype(F32) * yb).astype(BF16)
        for c in range(d // PROJ_CHUNK):
            cols = slice(c * PROJ_CHUNK, (c + 1) * PROJ_CHUNK)
            hres[:, cols] = h_ref[:, cols] + _dot(m_ref[...], wo_ref[:, cols])
        hn[...] = _rms(hres[...], n2_ref[...]).astype(BF16)

        stage_conv_inputs()
        n_chunks = wg_ref.shape[1] // FF_CHUNK
        for j in range(max(n_chunks, n_groups)):
            if j < n_chunks:
                _swiglu_chunk(hn[...], wg_ref, wu_ref, act_ref, j)
            if j < n_groups:
                conv_group(min(j * CONV_GROUP, tm - CONV_GROUP))

        h = hres[...] + 0.5 * _dot(act_ref[...], wd_ref[...])
        o_ref[...] = _rms(h, nf_ref[...]) if apply_final else h


def _params():
    return pltpu.CompilerParams(dimension_semantics=("arbitrary",), vmem_limit_bytes=VMEM_LIMIT)


def _ffn1_proj(x, n1, wg, wu, wd, nm, win, tm):
    t, d = x.shape
    d_ff = wg.shape[1]
    row = pl.BlockSpec((tm, d), lambda i: (i, 0))
    vec = _resident((1, d))
    return pl.pallas_call(
        _ffn1_proj_kernel,
        grid=(t // tm,),
        in_specs=[row, vec, _resident(wg.shape), _resident(wu.shape), _resident(wd.shape), vec,
                  _resident(win.shape)],
        out_specs=[row] * 6,
        out_shape=[jax.ShapeDtypeStruct((t, d), F32)] + [jax.ShapeDtypeStruct((t, d), BF16)] * 5,
        scratch_shapes=[pltpu.VMEM((tm, d_ff), BF16)],
        compiler_params=_params(),
        name="ffn1_proj",
    )(x, n1, wg, wu, wd, nm, win)


def _mix_ffn2(h, a, cx, bb, sga, sgb, wa, ba, lng, lnb, wao, wb, wbo, wo, n2, wg, wu, wd, nf,
              seq, tm, apply_final):
    t, d = h.shape
    d_ff = wg.shape[1]
    n_tiles = t // tm
    conv_row = pl.BlockSpec((tm, d), lambda s: (jnp.minimum(s, n_tiles - 1), 0))
    halo = pl.BlockSpec(
        (HALO, d), lambda s: (jnp.maximum(jnp.minimum(s, n_tiles - 1) * (tm // HALO) - 1, 0), 0))
    ffn_row = pl.BlockSpec((tm, d), lambda s: (jnp.maximum(s - 1, 0), 0))
    vec = _resident((1, d))
    return pl.pallas_call(
        functools.partial(_mix_ffn2_kernel, n_tiles, seq // tm, apply_final),
        grid=(n_tiles + 1,),
        in_specs=[ffn_row, ffn_row, ffn_row, conv_row, halo, conv_row, halo, conv_row,
                  _resident(wa.shape), vec, vec, vec, _resident(wao.shape),
                  _resident(wb.shape), _resident(wbo.shape), _resident(wo.shape),
                  vec, _resident(wg.shape), _resident(wu.shape), _resident(wd.shape), vec],
        out_specs=ffn_row,
        out_shape=jax.ShapeDtypeStruct((t, d), F32),
        scratch_shapes=[pltpu.VMEM((d // LANES, HALO + tm, LANES), F32),
                        pltpu.VMEM((d // LANES, HALO + tm, LANES), F32),
                        pltpu.VMEM((CONV_GROUP, d), F32),
                        pltpu.VMEM((tm, d), BF16),
                        pltpu.VMEM((tm, d), BF16),
                        pltpu.VMEM((tm, d), BF16),
                        pltpu.VMEM((tm, d), F32),
                        pltpu.VMEM((tm, d), BF16),
                        pltpu.VMEM((tm, d_ff), BF16)],
        compiler_params=_params(),
        name="mix_ffn2",
    )(h, sga, sgb, a, a, cx, cx, bb, wa, ba, lng, lnb, wao, wb, wbo, wo, n2, wg, wu, wd, nf)


def kernel(x, ffn1_norm, ffn1_w_gate, ffn1_w_up, ffn1_w_down, mix_norm, w_in, a_dw_w, a_dw_b, a_ln_g,
           a_ln_b, a_w_out, b_conv_w, b_w_out, w_o, ffn2_norm, ffn2_w_gate, ffn2_w_up, ffn2_w_down,
           final_norm):
    batch, seq, d = x.shape
    depth = w_in.shape[0]
    tm = 512
    assert a_dw_w.shape[1] - 1 <= HALO and b_conv_w.shape[1] - 1 <= HALO
    assert seq % tm == 0 and tm % HALO == 0 and tm % CONV_ROWS == 0 and d % LANES == 0

    def vec(v):
        return v.reshape(1, d).astype(F32)

    h = x.reshape(batch * seq, d)
    for l in range(depth):
        h, a, cx, bb, sga, sgb = _ffn1_proj(
            h, vec(ffn1_norm[l]), ffn1_w_gate[l].astype(BF16), ffn1_w_up[l].astype(BF16),
            ffn1_w_down[l].astype(BF16), vec(mix_norm[l]), w_in[l].astype(BF16), tm)
        h = _mix_ffn2(h, a, cx, bb, sga, sgb, a_dw_w[l], vec(a_dw_b[l]), vec(a_ln_g[l]), vec(a_ln_b[l]),
                      a_w_out[l].astype(BF16), b_conv_w[l], b_w_out[l].astype(BF16), w_o[l].astype(BF16),
                      vec(ffn2_norm[l]), ffn2_w_gate[l].astype(BF16), ffn2_w_up[l].astype(BF16),
                      ffn2_w_down[l].astype(BF16), vec(final_norm), seq, tm, apply_final=l == depth - 1)
    return h.reshape(batch, seq, d)
```
